```python
import jax, jax.numpy as jnp
from jax import lax
import numpy as np

D_MODEL = 1024
BATCH = 8
SEQ = 4096
DEPTH = 2

N_MIXERS = 2
N_LRU = (DEPTH + 1) // 2
N_FOX = DEPTH // 2
D_RNN = D_MODEL
LRU_BLOCKS = 16
LRU_BLOCK_DIM = D_RNN // LRU_BLOCKS
CONV_WIDTH = 4
LRU_C = 8.0
N_HEADS = 16
HEAD_DIM = D_MODEL // N_HEADS
Q_BLOCK = 128
D_FF = 4 * D_MODEL
EPS = 1e-6
NEG_INF = -1e30

kernel_name = "hawk_fox_interleaved_trunk"


def rms_norm(x, g):
    xf = x.astype(jnp.float32)
    y = xf * lax.rsqrt(jnp.mean(xf * xf, axis=-1, keepdims=True) + EPS)
    return (y * g.astype(jnp.float32)).astype(x.dtype)


def causal_depthwise_conv(x, w, b):
    S = x.shape[1]
    xp = jnp.pad(x, ((0, 0), (CONV_WIDTH - 1, 0), (0, 0)))
    y = b
    for k in range(CONV_WIDTH):
        y = y + xp[:, k:k + S] * w[k]
    return y


def rglru_mixer(h, w_in, conv_w, conv_b, w_r, b_r, w_i, b_i, lam, w_out):
    B, S, _ = h.shape
    u = h @ w_in
    gate_branch, x_branch = jnp.split(u, 2, axis=-1)
    gate_branch = jax.nn.gelu(gate_branch)
    xc = causal_depthwise_conv(x_branch, conv_w, conv_b)
    xb = xc.reshape(B, S, LRU_BLOCKS, LRU_BLOCK_DIM)
    r = jax.nn.sigmoid((jnp.einsum('bsnd,nde->bsne', xb, w_r) + b_r).astype(jnp.float32))
    i = jax.nn.sigmoid((jnp.einsum('bsnd,nde->bsne', xb, w_i) + b_i).astype(jnp.float32))
    r = r.reshape(B, S, D_RNN)
    i = i.reshape(B, S, D_RNN)
    log_a = LRU_C * r * jax.nn.log_sigmoid(lam.astype(jnp.float32))
    a = jnp.exp(log_a)
    mult = jnp.sqrt(-jnp.expm1(2.0 * log_a))
    bt = mult * (i * xc.astype(jnp.float32))

    def combine(left, right):
        a_l, b_l = left
        a_r, b_r_ = right
        return a_l * a_r, a_r * b_l + b_r_

    _, hs = lax.associative_scan(combine, (a, bt), axis=1)
    y = gate_branch * hs.astype(h.dtype)
    return y @ w_out


def fox_mixer(h, w_in, b_f, q_gain, k_gain, w_out):
    B, S, _ = h.shape
    u = h @ w_in
    q, k, v, f_logit = jnp.split(u, [D_MODEL, 2 * D_MODEL, 3 * D_MODEL], axis=-1)

    def heads(t):
        return t.reshape(B, S, N_HEADS, HEAD_DIM).transpose(0, 2, 1, 3)

    q = rms_norm(heads(q), q_gain)
    k = rms_norm(heads(k), k_gain)
    v = heads(v)
    log_f = jax.nn.log_sigmoid((f_logit + b_f).astype(jnp.float32))
    c = jnp.cumsum(log_f, axis=1).transpose(0, 2, 1)
    scale = HEAD_DIM ** -0.5
    nb = S // Q_BLOCK
    q_blocks = q.reshape(B, N_HEADS, nb, Q_BLOCK, HEAD_DIM).transpose(2, 0, 1, 3, 4)
    c_blocks = c.reshape(B, N_HEADS, nb, Q_BLOCK).transpose(2, 0, 1, 3)
    k_pos = jnp.arange(S)

    def attend(args):
        qb, cb, blk = args
        q_pos = blk * Q_BLOCK + jnp.arange(Q_BLOCK)
        s = jnp.einsum('bhqd,bhkd->bhqk', qb, k).astype(jnp.float32) * scale
        s = s + cb[..., :, None] - c[:, :, None, :]
        s = jnp.where(k_pos[None, :] <= q_pos[:, None], s, NEG_INF)
        p = jax.nn.softmax(s, axis=-1)
        return jnp.einsum('bhqk,bhkd->bhqd', p.astype(v.dtype), v)

    o = lax.map(attend, (q_blocks, c_blocks, jnp.arange(nb)))
    o = o.transpose(1, 0, 3, 2, 4).reshape(B, S, D_MODEL)
    return o @ w_out


def sq_relu_mlp(h, w1, w2):
    return jnp.square(jax.nn.relu(h @ w1)) @ w2


def setup_inputs(seed: int = 0) -> dict:
    key = jax.random.key(seed)
    ks = jax.random.split(key, 20)
    f32 = jnp.float32
    nrm = lambda k, shape, fan_in: jax.random.normal(k, shape, f32) * (fan_in ** -0.5)
    a0 = jax.random.uniform(ks[12], (N_LRU, D_RNN), f32, minval=0.9, maxval=0.999)
    return {
        "x": jax.random.normal(ks[0], (BATCH, SEQ, D_MODEL), f32),
        "mix_norm": 1.0 + 0.02 * jax.random.normal(ks[1], (DEPTH, D_MODEL), f32),
        "mlp_norm": 1.0 + 0.02 * jax.random.normal(ks[2], (DEPTH, D_MODEL), f32),
        "mlp_w1": nrm(ks[3], (DEPTH, D_MODEL, D_FF), D_MODEL),
        "mlp_w2": nrm(ks[4], (DEPTH, D_FF, D_MODEL), D_FF),
        "lru_w_in": nrm(ks[5], (N_LRU, D_MODEL, 2 * D_RNN), D_MODEL),
        "lru_conv_w": nrm(ks[6], (N_LRU, CONV_WIDTH, D_RNN), CONV_WIDTH),
        "lru_conv_b": 0.02 * jax.random.normal(ks[7], (N_LRU, D_RNN), f32),
        "lru_w_r": nrm(ks[8], (N_LRU, LRU_BLOCKS, LRU_BLOCK_DIM, LRU_BLOCK_DIM), LRU_BLOCK_DIM),
        "lru_b_r": 0.1 * jax.random.normal(ks[9], (N_LRU, LRU_BLOCKS, LRU_BLOCK_DIM), f32),
        "lru_w_i": nrm(ks[10], (N_LRU, LRU_BLOCKS, LRU_BLOCK_DIM, LRU_BLOCK_DIM), LRU_BLOCK_DIM),
        "lru_b_i": 0.1 * jax.random.normal(ks[11], (N_LRU, LRU_BLOCKS, LRU_BLOCK_DIM), f32),
        "lru_lambda": jnp.log(a0) - jnp.log1p(-a0),
        "lru_w_out": nrm(ks[13], (N_LRU, D_RNN, D_MODEL), D_RNN),
        "fox_w_in": nrm(ks[14], (N_FOX, D_MODEL, 3 * D_MODEL + N_HEADS), D_MODEL),
        "fox_b_f": jax.random.uniform(ks[15], (N_FOX, N_HEADS), f32, minval=1.0, maxval=6.0),
        "fox_q_gain": 1.0 + 0.02 * jax.random.normal(ks[16], (N_FOX, HEAD_DIM), f32),
        "fox_k_gain": 1.0 + 0.02 * jax.random.normal(ks[17], (N_FOX, HEAD_DIM), f32),
        "fox_w_out": nrm(ks[18], (N_FOX, D_MODEL, D_MODEL), D_MODEL),
    }


def reference(x, mix_norm, mlp_norm, mlp_w1, mlp_w2, lru_w_in, lru_conv_w, lru_conv_b,
              lru_w_r, lru_b_r, lru_w_i, lru_b_i, lru_lambda, lru_w_out,
              fox_w_in, fox_b_f, fox_q_gain, fox_k_gain, fox_w_out):
    for layer in range(DEPTH):
        h = rms_norm(x, mix_norm[layer])
        j = layer // N_MIXERS
        if layer % N_MIXERS == 0:
            mixed = rglru_mixer(h, lru_w_in[j], lru_conv_w[j], lru_conv_b[j], lru_w_r[j],
                                lru_b_r[j], lru_w_i[j], lru_b_i[j], lru_lambda[j], lru_w_out[j])
        else:
            mixed = fox_mixer(h, fox_w_in[j], fox_b_f[j], fox_q_gain[j], fox_k_gain[j],
                              fox_w_out[j])
        x = x + mixed
        x = x + sq_relu_mlp(rms_norm(x, mlp_norm[layer]), mlp_w1[layer], mlp_w2[layer])
    return x
```

```python
import functools

import jax
import jax.numpy as jnp
from jax import lax
from jax.experimental import pallas as pl
from jax.experimental.pallas import tpu as pltpu

F32 = jnp.float32
BF16 = jnp.bfloat16

EPS = 1e-6
NEG_INF = -1e30
N_HEADS = 16
HEAD_DIM = 64
LRU_BLOCKS = 16
CONV_WIDTH = 4
LRU_C = 8.0

V7X_SUBLANES = 8
V7X_LANES = 128
V7X_MXU_DIM = 256
V7X_VMEM_LIMIT_BYTES = 56 * 1024 * 1024

HEADS_PER_STEP = V7X_LANES // HEAD_DIM
HEADS_PER_MXU_TILE = V7X_MXU_DIM // HEAD_DIM


def _rms_norm(x, g):
    return x * lax.rsqrt(jnp.mean(x * x, axis=-1, keepdims=True) + EPS) * g


def _gelu_tanh(x):
    return 0.5 * x * (1.0 + jnp.tanh(0.7978845608028654 * (x + 0.044715 * (x * x * x))))


def _log_sigmoid(x):
    return jnp.minimum(x, 0.0) - jnp.log1p(jnp.exp(-jnp.abs(x)))


def _dot(a, b):
    return jnp.dot(a, b, preferred_element_type=F32)


def _resident(shape):
    zeros = (0,) * len(shape)
    return pl.BlockSpec(shape, lambda *_: zeros, pipeline_mode=pl.Buffered(1))


def _lru_kernel(x_ref, g_ref, win_ref, cw_ref, cb_ref, wg_ref, bg_ref, lam_ref, y_ref,
                xb_scr, a_scr, b_scr, h_scr, *, ts, d):
    @pl.when(pl.program_id(1) == 0)
    def _():
        xb_scr[0:V7X_SUBLANES, :] = jnp.zeros((V7X_SUBLANES, d), F32)
        h_scr[...] = jnp.zeros((V7X_SUBLANES, d), F32)

    h = _rms_norm(x_ref[0], g_ref[...]).astype(BF16)
    u = _dot(h, win_ref[...])
    gate = _gelu_tanh(u[:, :d])
    xb = u[:, d:]

    xb_scr[V7X_SUBLANES:V7X_SUBLANES + ts, :] = xb
    xc = cb_ref[...] + cw_ref[CONV_WIDTH - 1:CONV_WIDTH, :] * xb
    for back in range(1, CONV_WIDTH):
        k = CONV_WIDTH - 1 - back
        xc = xc + cw_ref[k:k + 1, :] * xb_scr[V7X_SUBLANES - back:V7X_SUBLANES - back + ts, :]
    xb_scr[0:V7X_SUBLANES, :] = xb[ts - V7X_SUBLANES:, :]

    xcb = xc.astype(BF16)
    r_parts, i_parts = [], []
    for j in range(d // V7X_MXU_DIM):
        lo = j * V7X_MXU_DIM
        g = _dot(xcb[:, lo:lo + V7X_MXU_DIM], wg_ref[j])
        r_parts.append(g[:, :V7X_MXU_DIM])
        i_parts.append(g[:, V7X_MXU_DIM:])
    r = jax.nn.sigmoid(jnp.concatenate(r_parts, axis=1) + bg_ref[0:1, :])
    i = jax.nn.sigmoid(jnp.concatenate(i_parts, axis=1) + bg_ref[1:2, :])

    log_a = (LRU_C * r) * _log_sigmoid(lam_ref[...])
    a = jnp.exp(log_a)
    mult = jnp.sqrt(-jnp.tanh(log_a) * (1.0 + a * a))
    b = mult * (i * xc)

    row = lax.broadcasted_iota(jnp.int32, (ts, d), 0) & (V7X_SUBLANES - 1)
    step = 1
    while step < V7X_SUBLANES:
        keep = row >= step
        b = jnp.where(keep, a * pltpu.roll(b, step, 0) + b, b)
        a = jnp.where(keep, a * pltpu.roll(a, step, 0), a)
        step *= 2
    a_scr[...] = a
    b_scr[...] = b

    def carry_group(gidx, h_prev):
        rows = pl.ds(pl.multiple_of(gidx * V7X_SUBLANES, V7X_SUBLANES), V7X_SUBLANES)
        hs = a_scr[rows, :] * h_prev + b_scr[rows, :]
        b_scr[rows, :] = hs
        return jnp.broadcast_to(hs[V7X_SUBLANES - 1:, :], (V7X_SUBLANES, d))

    h_scr[...] = lax.fori_loop(0, ts // V7X_SUBLANES, carry_group, h_scr[...], unroll=4)
    y_ref[0] = (gate * b_scr[...]).astype(BF16)


def _lru_mixer(x, g, w_in, conv_w, conv_b, w_gates, b_gates, lam, *, ts):
    bsz, seq, d = x.shape
    kern = functools.partial(_lru_kernel, ts=ts, d=d)
    return pl.pallas_call(
        kern,
        grid=(bsz, seq // ts),
        in_specs=[
            pl.BlockSpec((1, ts, d), lambda b, s: (b, s, 0)),
            _resident(g.shape), _resident(w_in.shape), _resident(conv_w.shape),
            _resident(conv_b.shape), _resident(w_gates.shape), _resident(b_gates.shape),
            _resident(lam.shape),
        ],
        out_specs=pl.BlockSpec((1, ts, d), lambda b, s: (b, s, 0)),
        out_shape=jax.ShapeDtypeStruct((bsz, seq, d), BF16),
        scratch_shapes=[
            pltpu.VMEM((ts + V7X_SUBLANES, d), F32),
            pltpu.VMEM((ts, d), F32),
            pltpu.VMEM((ts, d), F32),
            pltpu.VMEM((V7X_SUBLANES, d), F32),
        ],
        compiler_params=pltpu.CompilerParams(
            dimension_semantics=("parallel", "arbitrary"),
            vmem_limit_bytes=V7X_VMEM_LIMIT_BYTES),
        name="lru_mixer",
    )(x, g, w_in, conv_w, conv_b, w_gates, b_gates, lam)


def _mlp_kernel(x_ref, y_ref, wout_ref, g_ref, w1_ref, w2_ref, o_ref, mid_scr, *, tf):
    x = x_ref[...] + _dot(y_ref[...], wout_ref[...])
    h = _rms_norm(x, g_ref[...]).astype(BF16)
    d_ff = w1_ref.shape[1]
    for c in range(d_ff // tf):
        mid = _dot(h, w1_ref[:, c * tf:(c + 1) * tf])
        mid_scr[:, c * tf:(c + 1) * tf] = jnp.square(jnp.maximum(mid, 0.0)).astype(BF16)
    o_ref[...] = x + _dot(mid_scr[...], w2_ref[...])


def _mlp_block(x, y, w_out, g, w1, w2, *, tm, tf):
    t, d = x.shape
    d_ff = w1.shape[1]
    kern = functools.partial(_mlp_kernel, tf=tf)
    return pl.pallas_call(
        kern,
        grid=(t // tm,),
        in_specs=[
            pl.BlockSpec((tm, d), lambda i: (i, 0)),
            pl.BlockSpec((tm, d), lambda i: (i, 0)),
            _resident(w_out.shape), _resident(g.shape), _resident(w1.shape), _resident(w2.shape),
        ],
        out_specs=pl.BlockSpec((tm, d), lambda i: (i, 0)),
        out_shape=jax.ShapeDtypeStruct((t, d), F32),
        scratch_shapes=[pltpu.VMEM((tm, d_ff), BF16)],
        compiler_params=pltpu.CompilerParams(
            dimension_semantics=("parallel",),
            vmem_limit_bytes=V7X_VMEM_LIMIT_BYTES),
        name="mlp_block",
    )(x, y, w_out, g, w1, w2)


def _fox_in_kernel(x_ref, g_ref, wqkv_ref, wf_ref, bf_ref, qg_ref, kg_ref,
                   q_ref, k_ref, v_ref, c_ref, ct_ref, carry_scr, *, ts, d):
    @pl.when(pl.program_id(1) == 0)
    def _():
        carry_scr[...] = jnp.zeros(carry_scr.shape, F32)

    h = _rms_norm(x_ref[0], g_ref[...]).astype(BF16)

    rid = lax.broadcasted_iota(jnp.int32, (V7X_MXU_DIM, V7X_MXU_DIM), 0) // HEAD_DIM
    cid = lax.broadcasted_iota(jnp.int32, (V7X_MXU_DIM, V7X_MXU_DIM), 1) // HEAD_DIM
    head_mean = jnp.where(rid == cid, 1.0 / HEAD_DIM, 0.0).astype(BF16)

    def head_norm(t, gain):
        sq = t * t
        hi = sq.astype(BF16)
        lo = (sq - hi.astype(F32)).astype(BF16)
        parts = []
        for j in range(d // V7X_MXU_DIM):
            cols = slice(j * V7X_MXU_DIM, (j + 1) * V7X_MXU_DIM)
            parts.append(_dot(hi[:, cols], head_mean) + _dot(lo[:, cols], head_mean))
        ms = jnp.concatenate(parts, axis=1)
        return t * lax.rsqrt(ms + EPS) * gain

    scale = HEAD_DIM ** -0.5
    q = head_norm(_dot(h, wqkv_ref[:, 0:d]), qg_ref[...])
    q_ref[0] = (q * scale).astype(BF16)
    k = head_norm(_dot(h, wqkv_ref[:, d:2 * d]), kg_ref[...])
    k_ref[0] = k.astype(BF16)
    v_ref[0] = _dot(h, wqkv_ref[:, 2 * d:3 * d]).astype(BF16)

    c = _log_sigmoid(_dot(h, wf_ref[...]) + bf_ref[...])
    row = lax.broadcasted_iota(jnp.int32, c.shape, 0)
    step = 1
    while step < ts:
        c = c + jnp.where(row >= step, pltpu.roll(c, step, 0), 0.0)
        step *= 2
    c = c + carry_scr[...]
    carry_scr[...] = c[ts - 1:, :]
    c_ref[0] = c
    ct_ref[0] = c.T[0:N_HEADS, :]


def _fox_in(x, g, wqkv, wf, bf, qg, kg, *, ts):
    bsz, seq, d = x.shape
    kern = functools.partial(_fox_in_kernel, ts=ts, d=d)
    tok = pl.BlockSpec((1, ts, d), lambda b, s: (b, s, 0))
    return pl.pallas_call(
        kern,
        grid=(bsz, seq // ts),
        in_specs=[tok, _resident(g.shape), _resident(wqkv.shape), _resident(wf.shape),
                  _resident(bf.shape), _resident(qg.shape), _resident(kg.shape)],
        out_specs=[tok, tok, tok,
                   pl.BlockSpec((1, ts, V7X_LANES), lambda b, s: (b, s, 0)),
                   pl.BlockSpec((1, N_HEADS, ts), lambda b, s: (b, 0, s))],
        out_shape=[jax.ShapeDtypeStruct((bsz, seq, d), BF16)] * 3 + [
            jax.ShapeDtypeStruct((bsz, seq, V7X_LANES), F32),
            jax.ShapeDtypeStruct((bsz, N_HEADS, seq), F32)],
        scratch_shapes=[pltpu.VMEM((1, V7X_LANES), F32)],
        compiler_params=pltpu.CompilerParams(
            dimension_semantics=("parallel", "arbitrary"),
            vmem_limit_bytes=V7X_VMEM_LIMIT_BYTES),
        name="fox_in",
    )(x, g, wqkv, wf, bf, qg, kg)


def _fox_attn_kernel(q_ref, k_ref, v_ref, c_ref, ct_ref, o_ref, *, tq):
    hp = pl.program_id(1)
    qi = pl.program_id(2)
    q = q_ref[0]
    c_rows = c_ref[0]
    lane = lax.broadcasted_iota(jnp.int32, (tq, V7X_LANES), 1)
    causal = (lax.broadcasted_iota(jnp.int32, (tq, tq), 1)
              <= lax.broadcasted_iota(jnp.int32, (tq, tq), 0))

    outs = []
    for hh in range(HEADS_PER_STEP):
        head = hp * HEADS_PER_STEP + hh
        in_head = (lane >= hh * HEAD_DIM) & (lane < (hh + 1) * HEAD_DIM)
        qh = jnp.where(in_head, q, jnp.zeros_like(q))
        cq = jnp.sum(jnp.where(lane == head, c_rows, 0.0), axis=1, keepdims=True)

        def kv_step(j, carry, *, diagonal, qh=qh, cq=cq, head=head):
            m, l, acc = carry
            rows = pl.ds(pl.multiple_of(j * tq, tq), tq)
            s = lax.dot_general(qh, k_ref[0, rows, :], (((1,), (1,)), ((), ())),
                                preferred_element_type=F32)
            s = s + (cq - ct_ref[0, pl.ds(head, 1), rows])
            if diagonal:
                s = jnp.where(causal, s, NEG_INF)
            m_new = jnp.maximum(m, jnp.max(s, axis=1, keepdims=True))
            p = jnp.exp(s - m_new)
            alpha = jnp.exp(m - m_new)
            l = alpha * l + jnp.sum(p, axis=1, keepdims=True)
            acc = alpha * acc + _dot(p.astype(BF16), v_ref[0, rows, :])
            return m_new, l, acc

        init = (jnp.full((tq, 1), NEG_INF, F32), jnp.zeros((tq, 1), F32),
                jnp.zeros((tq, V7X_LANES), F32))
        carry = lax.fori_loop(0, qi, functools.partial(kv_step, diagonal=False), init)
        _, l, acc = kv_step(qi, carry, diagonal=True)
        outs.append(acc / l)

    o_ref[0] = jnp.where(lane < HEAD_DIM, outs[0], outs[1]).astype(BF16)


def _fox_attn(q, k, v, c, ct, *, tq):
    bsz, seq, d = q.shape
    kern = functools.partial(_fox_attn_kernel, tq=tq)
    q_spec = pl.BlockSpec((1, tq, V7X_LANES), lambda b, hp, i: (b, i, hp))
    kv_spec = pl.BlockSpec((1, seq, V7X_LANES), lambda b, hp, i: (b, 0, hp))
    return pl.pallas_call(
        kern,
        grid=(bsz, d // V7X_LANES, seq // tq),
        in_specs=[q_spec, kv_spec, kv_spec,
                  pl.BlockSpec((1, tq, V7X_LANES), lambda b, hp, i: (b, i, 0)),
                  pl.BlockSpec((1, N_HEADS, seq), lambda b, hp, i: (b, 0, 0))],
        out_specs=q_spec,
        out_shape=jax.ShapeDtypeStruct((bsz, seq, d), BF16),
        compiler_params=pltpu.CompilerParams(
            dimension_semantics=("parallel", "parallel", "arbitrary"),
            vmem_limit_bytes=V7X_VMEM_LIMIT_BYTES),
        name="fox_attn",
    )(q, k, v, c, ct)


def _block_diag_tiles(w):
    n_tiles = LRU_BLOCKS // HEADS_PER_MXU_TILE
    w4 = w.reshape(n_tiles, HEADS_PER_MXU_TILE, HEAD_DIM, HEAD_DIM)
    eye = jnp.eye(HEADS_PER_MXU_TILE, dtype=w.dtype)
    return jnp.einsum("jade,ab->jadbe", w4, eye).reshape(n_tiles, V7X_MXU_DIM, V7X_MXU_DIM)


def kernel(x, mix_norm, mlp_norm, mlp_w1, mlp_w2, lru_w_in, lru_conv_w, lru_conv_b, lru_w_r, lru_b_r, lru_w_i, lru_b_i, lru_lambda, lru_w_out, fox_w_in, fox_b_f, fox_q_gain, fox_k_gain, fox_w_out):
    bsz, seq, d = x.shape
    t = bsz * seq
    assert d == N_HEADS * HEAD_DIM and lru_w_r.shape[1:] == (LRU_BLOCKS, HEAD_DIM, HEAD_DIM)
    ts = min(256, seq)
    tq = min(512, seq)
    tm = min(512, t)
    tf = 1024
    assert seq % ts == 0 and seq % tq == 0 and t % tm == 0 and mlp_w1.shape[2] % tf == 0

    row = lambda v: v.reshape(1, -1).astype(F32)

    w_gates = jnp.concatenate(
        [_block_diag_tiles(lru_w_r[0]), _block_diag_tiles(lru_w_i[0])], axis=2).astype(BF16)
    b_gates = jnp.stack([lru_b_r[0].reshape(-1), lru_b_i[0].reshape(-1)]).astype(F32)
    y = _lru_mixer(x, row(mix_norm[0]), lru_w_in[0].astype(BF16), lru_conv_w[0].astype(F32),
                   row(lru_conv_b[0]), w_gates, b_gates, row(lru_lambda[0]), ts=ts)
    x = _mlp_block(x.reshape(t, d), y.reshape(t, d), lru_w_out[0].astype(BF16),
                   row(mlp_norm[0]), mlp_w1[0].astype(BF16), mlp_w2[0].astype(BF16),
                   tm=tm, tf=tf).reshape(bsz, seq, d)

    w_in = fox_w_in[0]
    wf = jnp.pad(w_in[:, 3 * d:], ((0, 0), (0, V7X_LANES - N_HEADS))).astype(BF16)
    bf = jnp.pad(fox_b_f[0], (0, V7X_LANES - N_HEADS)).reshape(1, -1).astype(F32)
    q, k, v, c, ct = _fox_in(x, row(mix_norm[1]), w_in[:, :3 * d].astype(BF16), wf, bf,
                             row(jnp.tile(fox_q_gain[0], N_HEADS)),
                             row(jnp.tile(fox_k_gain[0], N_HEADS)), ts=ts)
    o = _fox_attn(q, k, v, c, ct, tq=tq)
    x = _mlp_block(x.reshape(t, d), o.reshape(t, d), fox_w_out[0].astype(BF16),
                   row(mlp_norm[1]), mlp_w1[1].astype(BF16), mlp_w2[1].astype(BF16),
                   tm=tm, tf=tf).reshape(bsz, seq, d)
    return x
```

```python
import functools

import jax
import jax.numpy as jnp
from jax import lax
from jax.experimental import pallas as pl
from jax.experimental.pallas import tpu as pltpu

F32 = jnp.float32
BF16 = jnp.bfloat16

EPS = 1e-6
NEG_INF = -1e30
N_HEADS = 16
HEAD_DIM = 64
LRU_BLOCKS = 16
CONV_WIDTH = 4
LRU_C = 8.0
LOG2_E = 1.4426950408889634
BIAS_PIECES = 3
BIAS_ROWS = 2 * BIAS_PIECES

V7X_SUBLANES = 8
V7X_LANES = 128
V7X_MXU_DIM = 256
V7X_VMEM_LIMIT_BYTES = 56 * 1024 * 1024

HEADS_PER_STEP = V7X_LANES // HEAD_DIM
HEADS_PER_MXU_TILE = V7X_MXU_DIM // HEAD_DIM


def _rms_norm(x, g):
    return x * lax.rsqrt(jnp.mean(x * x, axis=-1, keepdims=True) + EPS) * g


def _gelu_tanh(x):
    return 0.5 * x * (1.0 + jnp.tanh(0.7978845608028654 * (x + 0.044715 * (x * x * x))))


def _log_sigmoid(x):
    return jnp.minimum(x, 0.0) - jnp.log1p(jnp.exp(-jnp.abs(x)))


def _dot(a, b):
    return jnp.dot(a, b, preferred_element_type=F32)


def _resident(shape):
    zeros = (0,) * len(shape)
    return pl.BlockSpec(shape, lambda *_: zeros, pipeline_mode=pl.Buffered(1))


def _lru_kernel(x_ref, g_ref, win_ref, cw_ref, cb_ref, wg_ref, bg_ref, lam_ref, y_ref,
                xb_scr, a_scr, b_scr, h_scr, *, ts, d):
    @pl.when(pl.program_id(1) == 0)
    def _():
        xb_scr[0:V7X_SUBLANES, :] = jnp.zeros((V7X_SUBLANES, d), F32)
        h_scr[...] = jnp.zeros((V7X_SUBLANES, d), F32)

    h = _rms_norm(x_ref[0], g_ref[...]).astype(BF16)
    u = _dot(h, win_ref[...])
    gate = _gelu_tanh(u[:, :d])
    xb = u[:, d:]

    xb_scr[V7X_SUBLANES:V7X_SUBLANES + ts, :] = xb
    xc = cb_ref[...] + cw_ref[CONV_WIDTH - 1:CONV_WIDTH, :] * xb
    for back in range(1, CONV_WIDTH):
        k = CONV_WIDTH - 1 - back
        xc = xc + cw_ref[k:k + 1, :] * xb_scr[V7X_SUBLANES - back:V7X_SUBLANES - back + ts, :]
    xb_scr[0:V7X_SUBLANES, :] = xb[ts - V7X_SUBLANES:, :]

    xcb = xc.astype(BF16)
    r_parts, i_parts = [], []
    for j in range(d // V7X_MXU_DIM):
        lo = j * V7X_MXU_DIM
        g = _dot(xcb[:, lo:lo + V7X_MXU_DIM], wg_ref[j])
        r_parts.append(g[:, :V7X_MXU_DIM])
        i_parts.append(g[:, V7X_MXU_DIM:])
    r = jax.nn.sigmoid(jnp.concatenate(r_parts, axis=1) + bg_ref[0:1, :])
    i = jax.nn.sigmoid(jnp.concatenate(i_parts, axis=1) + bg_ref[1:2, :])

    log_a = (LRU_C * r) * _log_sigmoid(lam_ref[...])
    a = jnp.exp(log_a)
    mult = jnp.sqrt(-jnp.tanh(log_a) * (1.0 + a * a))
    b = mult * (i * xc)

    row = lax.broadcasted_iota(jnp.int32, (ts, d), 0) & (V7X_SUBLANES - 1)
    step = 1
    while step < V7X_SUBLANES:
        keep = row >= step
        b = jnp.where(keep, a * pltpu.roll(b, step, 0) + b, b)
        a = jnp.where(keep, a * pltpu.roll(a, step, 0), a)
        step *= 2
    a_scr[...] = a
    b_scr[...] = b

    def carry_group(gidx, h_prev):
        rows = pl.ds(pl.multiple_of(gidx * V7X_SUBLANES, V7X_SUBLANES), V7X_SUBLANES)
        hs = a_scr[rows, :] * h_prev + b_scr[rows, :]
        b_scr[rows, :] = hs
        return jnp.broadcast_to(hs[V7X_SUBLANES - 1:, :], (V7X_SUBLANES, d))

    h_scr[...] = lax.fori_loop(0, ts // V7X_SUBLANES, carry_group, h_scr[...], unroll=4)
    y_ref[0] = (gate * b_scr[...]).astype(BF16)


def _lru_mixer(x, g, w_in, conv_w, conv_b, w_gates, b_gates, lam, *, ts):
    bsz, seq, d = x.shape
    kern = functools.partial(_lru_kernel, ts=ts, d=d)
    return pl.pallas_call(
        kern,
        grid=(bsz, seq // ts),
        in_specs=[
            pl.BlockSpec((1, ts, d), lambda b, s: (b, s, 0)),
            _resident(g.shape), _resident(w_in.shape), _resident(conv_w.shape),
            _resident(conv_b.shape), _resident(w_gates.shape), _resident(b_gates.shape),
            _resident(lam.shape),
        ],
        out_specs=pl.BlockSpec((1, ts, d), lambda b, s: (b, s, 0)),
        out_shape=jax.ShapeDtypeStruct((bsz, seq, d), BF16),
        scratch_shapes=[
            pltpu.VMEM((ts + V7X_SUBLANES, d), F32),
            pltpu.VMEM((ts, d), F32),
            pltpu.VMEM((ts, d), F32),
            pltpu.VMEM((V7X_SUBLANES, d), F32),
        ],
        compiler_params=pltpu.CompilerParams(
            dimension_semantics=("parallel", "arbitrary"),
            vmem_limit_bytes=V7X_VMEM_LIMIT_BYTES),
        name="lru_mixer",
    )(x, g, w_in, conv_w, conv_b, w_gates, b_gates, lam)


def _mlp_kernel(x_ref, y_ref, wout_ref, g_ref, w1_ref, w2_ref, o_ref, mid_scr, *, tf):
    x = x_ref[...] + _dot(y_ref[...], wout_ref[...])
    h = _rms_norm(x, g_ref[...]).astype(BF16)
    d_ff = w1_ref.shape[1]
    for c in range(d_ff // tf):
        mid = _dot(h, w1_ref[:, c * tf:(c + 1) * tf])
        mid_scr[:, c * tf:(c + 1) * tf] = jnp.square(jnp.maximum(mid, 0.0)).astype(BF16)
    o_ref[...] = x + _dot(mid_scr[...], w2_ref[...])


def _mlp_block(x, y, w_out, g, w1, w2, *, tm, tf):
    t, d = x.shape
    d_ff = w1.shape[1]
    kern = functools.partial(_mlp_kernel, tf=tf)
    return pl.pallas_call(
        kern,
        grid=(t // tm,),
        in_specs=[
            pl.BlockSpec((tm, d), lambda i: (i, 0)),
            pl.BlockSpec((tm, d), lambda i: (i, 0)),
            _resident(w_out.shape), _resident(g.shape), _resident(w1.shape), _resident(w2.shape),
        ],
        out_specs=pl.BlockSpec((tm, d), lambda i: (i, 0)),
        out_shape=jax.ShapeDtypeStruct((t, d), F32),
        scratch_shapes=[pltpu.VMEM((tm, d_ff), BF16)],
        compiler_params=pltpu.CompilerParams(
            dimension_semantics=("parallel",),
            vmem_limit_bytes=V7X_VMEM_LIMIT_BYTES),
        name="mlp_block",
    )(x, y, w_out, g, w1, w2)


def _fox_in_kernel(x_ref, g_ref, wqkv_ref, wf_ref, bf_ref, qg_ref, kg_ref,
                   qt_ref, k_ref, vt_ref, qb_ref, kb_ref, carry_scr, *, ts, d):
    @pl.when(pl.program_id(1) == 0)
    def _():
        carry_scr[...] = jnp.zeros(carry_scr.shape, F32)

    h = _rms_norm(x_ref[0], g_ref[...]).astype(BF16)

    rid = lax.broadcasted_iota(jnp.int32, (V7X_MXU_DIM, V7X_MXU_DIM), 0) // HEAD_DIM
    cid = lax.broadcasted_iota(jnp.int32, (V7X_MXU_DIM, V7X_MXU_DIM), 1) // HEAD_DIM
    head_mean = jnp.where(rid == cid, 1.0 / HEAD_DIM, 0.0).astype(BF16)

    def head_norm(t, gain):
        sq = t * t
        hi = sq.astype(BF16)
        lo = (sq - hi.astype(F32)).astype(BF16)
        parts = []
        for j in range(d // V7X_MXU_DIM):
            cols = slice(j * V7X_MXU_DIM, (j + 1) * V7X_MXU_DIM)
            parts.append(_dot(hi[:, cols], head_mean) + _dot(lo[:, cols], head_mean))
        ms = jnp.concatenate(parts, axis=1)
        return t * lax.rsqrt(ms + EPS) * gain

    q = head_norm(_dot(h, wqkv_ref[:, 0:d]), qg_ref[...])
    qt_ref[0] = (q * (HEAD_DIM ** -0.5 * LOG2_E)).T.astype(BF16)
    k = head_norm(_dot(h, wqkv_ref[:, d:2 * d]), kg_ref[...])
    k_ref[0] = k.astype(BF16)
    vt_ref[0] = _dot(h, wqkv_ref[:, 2 * d:3 * d]).T.astype(BF16)

    c = _log_sigmoid(_dot(h, wf_ref[...]) + bf_ref[...])
    row = lax.broadcasted_iota(jnp.int32, c.shape, 0)
    step = 1
    while step < ts:
        c = c + jnp.where(row >= step, pltpu.roll(c, step, 0), 0.0)
        step *= 2
    c = c + carry_scr[...]
    carry_scr[...] = c[ts - 1:, :]

    c2 = c * LOG2_E
    hi = c2.astype(BF16)
    mid = (c2 - hi.astype(F32)).astype(BF16)
    lo = (c2 - hi.astype(F32) - mid.astype(F32)).astype(BF16)
    pieces = jnp.concatenate([hi, mid, lo], axis=1)
    src = lax.broadcasted_iota(jnp.int32, (BIAS_PIECES * V7X_LANES, 2 * V7X_LANES), 0)
    dst = lax.broadcasted_iota(jnp.int32, (BIAS_PIECES * V7X_LANES, 2 * V7X_LANES), 1)
    piece, head = src // V7X_LANES, src % V7X_LANES
    q_slot = BIAS_ROWS * head + piece
    k_slot = V7X_LANES + BIAS_ROWS * head + BIAS_PIECES + piece
    place = jnp.where(head < N_HEADS,
                      jnp.where(dst == q_slot, 1.0, jnp.where(dst == k_slot, -1.0, 0.0)),
                      0.0).astype(BF16)
    lane = lax.broadcasted_iota(jnp.int32, (1, 2 * V7X_LANES), 1)
    slot = lane % V7X_LANES
    is_ones = (slot < BIAS_ROWS * N_HEADS) & (
        ((lane < V7X_LANES) & (slot % BIAS_ROWS >= BIAS_PIECES))
        | ((lane >= V7X_LANES) & (slot % BIAS_ROWS < BIAS_PIECES)))
    bias = _dot(pieces, place) + jnp.where(is_ones, 1.0, 0.0)
    qb_ref[0] = bias[:, :V7X_LANES].T.astype(BF16)
    kb_ref[0] = bias[:, V7X_LANES:].astype(BF16)


def _fox_in(x, g, wqkv, wf, bf, qg, kg, *, ts):
    bsz, seq, d = x.shape
    kern = functools.partial(_fox_in_kernel, ts=ts, d=d)
    tok = pl.BlockSpec((1, ts, d), lambda b, s: (b, s, 0))
    tok_t = pl.BlockSpec((1, d, ts), lambda b, s: (b, 0, s))
    return pl.pallas_call(
        kern,
        grid=(bsz, seq // ts),
        in_specs=[tok, _resident(g.shape), _resident(wqkv.shape), _resident(wf.shape),
                  _resident(bf.shape), _resident(qg.shape), _resident(kg.shape)],
        out_specs=[tok_t, tok, tok_t,
                   pl.BlockSpec((1, V7X_LANES, ts), lambda b, s: (b, 0, s)),
                   pl.BlockSpec((1, ts, V7X_LANES), lambda b, s: (b, s, 0))],
        out_shape=[jax.ShapeDtypeStruct((bsz, d, seq), BF16),
                   jax.ShapeDtypeStruct((bsz, seq, d), BF16),
                   jax.ShapeDtypeStruct((bsz, d, seq), BF16),
                   jax.ShapeDtypeStruct((bsz, V7X_LANES, seq), BF16),
                   jax.ShapeDtypeStruct((bsz, seq, V7X_LANES), BF16)],
        scratch_shapes=[pltpu.VMEM((1, V7X_LANES), F32)],
        compiler_params=pltpu.CompilerParams(
            dimension_semantics=("parallel", "arbitrary"),
            vmem_limit_bytes=V7X_VMEM_LIMIT_BYTES),
        name="fox_in",
    )(x, g, wqkv, wf, bf, qg, kg)


def _fox_attn_kernel(qt_ref, k_ref, kb_ref, vt_ref, qb_ref, o_ref, s_scr, *, tq, tk):
    hp = pl.program_id(1)
    qi = pl.program_id(2)
    qt = qt_ref[0]
    qb = qb_ref[0]
    row = lax.broadcasted_iota(jnp.int32, (V7X_LANES, tq), 0)
    zero = jnp.zeros_like(qt)
    rhs = []
    for hh in range(HEADS_PER_STEP):
        first = (hp * HEADS_PER_STEP + hh) * BIAS_ROWS
        q_rows = jnp.where((row >= hh * HEAD_DIM) & (row < (hh + 1) * HEAD_DIM), qt, zero)
        b_rows = jnp.where((row >= first) & (row < first + BIAS_ROWS), qb, zero)
        rhs.append(jnp.concatenate([q_rows, b_rows], axis=0))
    rhs = jnp.concatenate(rhs, axis=1)

    key_pos = lax.broadcasted_iota(jnp.int32, (tk, tq), 0)
    query_pos = lax.broadcasted_iota(jnp.int32, (tk, tq), 1)

    def scores(j):
        keys = pl.ds(pl.multiple_of(j * tk, tk), tk)
        lhs = jnp.concatenate([k_ref[0, keys, :], kb_ref[0, keys, :]], axis=1)
        return _dot(lhs, rhs)

    def softmax_pv(j, s, carry, *, key_offset=None):
        keys = pl.ds(pl.multiple_of(j * tk, tk), tk)
        out = []
        for hh in range(HEADS_PER_STEP):
            m, l, acc = carry[3 * hh:3 * hh + 3]
            sh = s[:, hh * tq:(hh + 1) * tq]
            if key_offset is not None:
                sh = jnp.where(key_pos + key_offset <= query_pos, sh, NEG_INF)
            m_new = jnp.maximum(m, jnp.max(sh, axis=0, keepdims=True))
            p = jnp.exp2(sh - m_new)
            alpha = jnp.exp2(m - m_new)
            l = alpha * l + jnp.sum(p, axis=0, keepdims=True)
            vt = vt_ref[0, hh * HEAD_DIM:(hh + 1) * HEAD_DIM, keys]
            acc = alpha * acc + _dot(vt, p.astype(BF16))
            out += [m_new, l, acc]
        return tuple(out)

    init = (jnp.full((1, tq), NEG_INF, F32), jnp.zeros((1, tq), F32),
            jnp.zeros((HEAD_DIM, tq), F32)) * HEADS_PER_STEP
    assert tq == 2 * tk
    n_full = 2 * qi
    s_scr[0] = scores(0)

    def tile_pair(jp, carry):
        j = 2 * jp
        s_scr[1] = scores(j + 1)
        carry = softmax_pv(j, s_scr.at[0], carry)
        s_scr[0] = scores(j + 2)
        return softmax_pv(j + 1, s_scr.at[1], carry)

    carry = lax.fori_loop(0, qi, tile_pair, init)
    s_scr[1] = scores(n_full + 1)
    carry = softmax_pv(n_full, s_scr.at[0], carry, key_offset=0)
    carry = softmax_pv(n_full + 1, s_scr.at[1], carry, key_offset=tk)
    o_t = jnp.concatenate([carry[3 * hh + 2] / carry[3 * hh + 1]
                           for hh in range(HEADS_PER_STEP)], axis=0)
    o_ref[0] = o_t.T.astype(BF16)


def _fox_attn(qt, k, kb, vt, qb, *, tq, tk):
    bsz, seq, d = k.shape
    kern = functools.partial(_fox_attn_kernel, tq=tq, tk=tk)
    return pl.pallas_call(
        kern,
        grid=(bsz, d // V7X_LANES, seq // tq),
        in_specs=[pl.BlockSpec((1, V7X_LANES, tq), lambda b, hp, i: (b, hp, i)),
                  pl.BlockSpec((1, seq, V7X_LANES), lambda b, hp, i: (b, 0, hp)),
                  pl.BlockSpec((1, seq, V7X_LANES), lambda b, hp, i: (b, 0, 0)),
                  pl.BlockSpec((1, V7X_LANES, seq), lambda b, hp, i: (b, hp, 0)),
                  pl.BlockSpec((1, V7X_LANES, tq), lambda b, hp, i: (b, 0, i))],
        out_specs=pl.BlockSpec((1, tq, V7X_LANES), lambda b, hp, i: (b, i, hp)),
        out_shape=jax.ShapeDtypeStruct((bsz, seq, d), BF16),
        scratch_shapes=[pltpu.VMEM((2, tk, HEADS_PER_STEP * tq), F32)],
        compiler_params=pltpu.CompilerParams(
            dimension_semantics=("parallel", "parallel", "arbitrary"),
            vmem_limit_bytes=V7X_VMEM_LIMIT_BYTES),
        name="fox_attn",
    )(qt, k, kb, vt, qb)


def _block_diag_tiles(w):
    n_tiles = LRU_BLOCKS // HEADS_PER_MXU_TILE
    w4 = w.reshape(n_tiles, HEADS_PER_MXU_TILE, HEAD_DIM, HEAD_DIM)
    eye = jnp.eye(HEADS_PER_MXU_TILE, dtype=w.dtype)
    return jnp.einsum("jade,ab->jadbe", w4, eye).reshape(n_tiles, V7X_MXU_DIM, V7X_MXU_DIM)


def kernel(x, mix_norm, mlp_norm, mlp_w1, mlp_w2, lru_w_in, lru_conv_w, lru_conv_b, lru_w_r, lru_b_r, lru_w_i, lru_b_i, lru_lambda, lru_w_out, fox_w_in, fox_b_f, fox_q_gain, fox_k_gain, fox_w_out):
    bsz, seq, d = x.shape
    t = bsz * seq
    assert d == N_HEADS * HEAD_DIM and lru_w_r.shape[1:] == (LRU_BLOCKS, HEAD_DIM, HEAD_DIM)
    ts = min(256, seq)
    tq = min(512, seq)
    tk = min(256, tq)
    tm = min(512, t)
    tf = 1024
    assert seq % ts == 0 and seq % tq == 0 and tq % tk == 0
    assert t % tm == 0 and mlp_w1.shape[2] % tf == 0

    row = lambda v: v.reshape(1, -1).astype(F32)

    w_gates = jnp.concatenate(
        [_block_diag_tiles(lru_w_r[0]), _block_diag_tiles(lru_w_i[0])], axis=2).astype(BF16)
    b_gates = jnp.stack([lru_b_r[0].reshape(-1), lru_b_i[0].reshape(-1)]).astype(F32)
    y = _lru_mixer(x, row(mix_norm[0]), lru_w_in[0].astype(BF16), lru_conv_w[0].astype(F32),
                   row(lru_conv_b[0]), w_gates, b_gates, row(lru_lambda[0]), ts=ts)
    x = _mlp_block(x.reshape(t, d), y.reshape(t, d), lru_w_out[0].astype(BF16),
                   row(mlp_norm[0]), mlp_w1[0].astype(BF16), mlp_w2[0].astype(BF16),
                   tm=tm, tf=tf).reshape(bsz, seq, d)

    w_in = fox_w_in[0]
    wf = jnp.pad(w_in[:, 3 * d:], ((0, 0), (0, V7X_LANES - N_HEADS))).astype(BF16)
    bf = jnp.pad(fox_b_f[0], (0, V7X_LANES - N_HEADS)).reshape(1, -1).astype(F32)
    qt, k, vt, qb, kb = _fox_in(x, row(mix_norm[1]), w_in[:, :3 * d].astype(BF16), wf, bf,
                                row(jnp.tile(fox_q_gain[0], N_HEADS)),
                                row(jnp.tile(fox_k_gain[0], N_HEADS)), ts=ts)
    o = _fox_attn(qt, k, kb, vt, qb, tq=tq, tk=tk)
    x = _mlp_block(x.reshape(t, d), o.reshape(t, d), fox_w_out[0].astype(BF16),
                   row(mlp_norm[1]), mlp_w1[1].astype(BF16), mlp_w2[1].astype(BF16),
                   tm=tm, tf=tf).reshape(bsz, seq, d)
    return x
```

```python
import functools

import jax
import jax.numpy as jnp
from jax import lax
from jax.experimental import pallas as pl
from jax.experimental.pallas import tpu as pltpu

F32 = jnp.float32
BF16 = jnp.bfloat16

EPS = 1e-6
NEG_INF = -1e30
N_HEADS = 16
HEAD_DIM = 64
LRU_BLOCKS = 16
CONV_WIDTH = 4
LRU_C = 8.0
LOG2_E = 1.4426950408889634
BIAS_PIECES = 3
BIAS_ROWS = 2 * BIAS_PIECES

V7X_SUBLANES = 8
BF16_SUBLANES = 2 * V7X_SUBLANES
V7X_LANES = 128
V7X_MXU_DIM = 256
V7X_VMEM_LIMIT_BYTES = 56 * 1024 * 1024

HEADS_PER_STEP = V7X_LANES // HEAD_DIM
HEADS_PER_MXU_TILE = V7X_MXU_DIM // HEAD_DIM


def _rms_norm(x, g):
    return x * lax.rsqrt(jnp.mean(x * x, axis=-1, keepdims=True) + EPS) * g


def _gelu_tanh(x):
    return 0.5 * x * (1.0 + jnp.tanh(0.7978845608028654 * (x + 0.044715 * (x * x * x))))


def _log_sigmoid(x):
    return jnp.minimum(x, 0.0) - jnp.log1p(jnp.exp(-jnp.abs(x)))


def _dot(a, b):
    return jnp.dot(a, b, preferred_element_type=F32)


def _resident(shape):
    zeros = (0,) * len(shape)
    return pl.BlockSpec(shape, lambda *_: zeros, pipeline_mode=pl.Buffered(1))


def _lru_kernel(x_ref, g_ref, win_ref, cw_ref, cb_ref, wg_ref, bg_ref, lam_ref, y_ref,
                xb_scr, h_scr, *, tt, d):
    nb = V7X_SUBLANES
    rows = nb * tt
    tail = (CONV_WIDTH - 1) * nb

    @pl.when(pl.program_id(1) == 0)
    def _():
        xb_scr[0:tail, :] = jnp.zeros((tail, d), F32)
        h_scr[...] = jnp.zeros((nb, d), F32)

    r_id = lax.broadcasted_iota(jnp.int32, (rows, rows), 0)
    c_id = lax.broadcasted_iota(jnp.int32, (rows, rows), 1)
    to_time_major = jnp.where(c_id == (r_id % nb) * tt + r_id // nb, 1.0, 0.0).astype(BF16)
    to_seq_major = jnp.where(r_id == (c_id % nb) * tt + c_id // nb, 1.0, 0.0).astype(BF16)

    h = _rms_norm(x_ref[...].reshape(rows, d), g_ref[...]).astype(BF16)
    h = _dot(to_time_major, h).astype(BF16)
    u = _dot(h, win_ref[...])
    gate = _gelu_tanh(u[:, :d])
    xb = u[:, d:]

    xb_scr[tail:tail + rows, :] = xb
    xc = cb_ref[...] + cw_ref[CONV_WIDTH - 1:CONV_WIDTH, :] * xb
    for back in range(1, CONV_WIDTH):
        k = CONV_WIDTH - 1 - back
        xc = xc + cw_ref[k:k + 1, :] * xb_scr[tail - back * nb:tail - back * nb + rows, :]
    xb_scr[0:tail, :] = xb[rows - tail:, :]

    xcb = xc.astype(BF16)
    r_parts, i_parts = [], []
    for j in range(d // V7X_MXU_DIM):
        lo = j * V7X_MXU_DIM
        g = _dot(xcb[:, lo:lo + V7X_MXU_DIM], wg_ref[j])
        r_parts.append(g[:, :V7X_MXU_DIM])
        i_parts.append(g[:, V7X_MXU_DIM:])
    r = jax.nn.sigmoid(jnp.concatenate(r_parts, axis=1) + bg_ref[0:1, :])
    i = jax.nn.sigmoid(jnp.concatenate(i_parts, axis=1) + bg_ref[1:2, :])

    neg_rate = -LRU_C * _log_sigmoid(lam_ref[...])
    neg_log_a = r * neg_rate
    a = jnp.exp2(r * (neg_rate * -LOG2_E))
    b = jnp.sqrt(jnp.tanh(neg_log_a) * (1.0 + a * a)) * (i * xc)

    state = h_scr[...]
    hs = []
    for t in range(tt):
        state = a[t * nb:(t + 1) * nb, :] * state + b[t * nb:(t + 1) * nb, :]
        hs.append(state)
    h_scr[...] = state
    y = (gate * jnp.concatenate(hs, axis=0)).astype(BF16)
    y_ref[...] = _dot(to_seq_major, y).astype(BF16).reshape(nb, tt, d)


def _lru_mixer(x, g, w_in, conv_w, conv_b, w_gates, b_gates, lam, *, tt):
    bsz, seq, d = x.shape
    nb = V7X_SUBLANES
    kern = functools.partial(_lru_kernel, tt=tt, d=d)
    return pl.pallas_call(
        kern,
        grid=(bsz // nb, seq // tt),
        in_specs=[
            pl.BlockSpec((nb, tt, d), lambda b, s: (b, s, 0)),
            _resident(g.shape), _resident(w_in.shape), _resident(conv_w.shape),
            _resident(conv_b.shape), _resident(w_gates.shape), _resident(b_gates.shape),
            _resident(lam.shape),
        ],
        out_specs=pl.BlockSpec((nb, tt, d), lambda b, s: (b, s, 0)),
        out_shape=jax.ShapeDtypeStruct((bsz, seq, d), BF16),
        scratch_shapes=[
            pltpu.VMEM(((tt + CONV_WIDTH - 1) * nb, d), F32),
            pltpu.VMEM((nb, d), F32),
        ],
        compiler_params=pltpu.CompilerParams(
            dimension_semantics=("parallel", "arbitrary"),
            vmem_limit_bytes=V7X_VMEM_LIMIT_BYTES),
        name="lru_mixer",
    )(x, g, w_in, conv_w, conv_b, w_gates, b_gates, lam)


def _mlp_kernel(x_ref, y_ref, wout_ref, g_ref, w1_ref, w2_ref, o_ref, mid_scr, *, tf):
    x = x_ref[...] + _dot(y_ref[...], wout_ref[...])
    h = _rms_norm(x, g_ref[...]).astype(BF16)
    d_ff = w1_ref.shape[1]
    for c in range(d_ff // tf):
        mid = _dot(h, w1_ref[:, c * tf:(c + 1) * tf])
        mid_scr[:, c * tf:(c + 1) * tf] = jnp.square(jnp.maximum(mid, 0.0)).astype(BF16)
    o_ref[...] = x + _dot(mid_scr[...], w2_ref[...])


def _mlp_block(x, y, w_out, g, w1, w2, *, tm, tf):
    t, d = x.shape
    d_ff = w1.shape[1]
    kern = functools.partial(_mlp_kernel, tf=tf)
    return pl.pallas_call(
        kern,
        grid=(t // tm,),
        in_specs=[
            pl.BlockSpec((tm, d), lambda i: (i, 0)),
            pl.BlockSpec((tm, d), lambda i: (i, 0)),
            _resident(w_out.shape), _resident(g.shape), _resident(w1.shape), _resident(w2.shape),
        ],
        out_specs=pl.BlockSpec((tm, d), lambda i: (i, 0)),
        out_shape=jax.ShapeDtypeStruct((t, d), F32),
        scratch_shapes=[pltpu.VMEM((tm, d_ff), BF16)],
        compiler_params=pltpu.CompilerParams(
            dimension_semantics=("parallel",),
            vmem_limit_bytes=V7X_VMEM_LIMIT_BYTES),
        name="mlp_block",
    )(x, y, w_out, g, w1, w2)


def _fox_in_kernel(x_ref, g_ref, wqkv_ref, wf_ref, bf_ref, qg_ref, kg_ref,
                   qt_ref, k_ref, vt_ref, qb_ref, kb_ref, carry_scr, *, ts, d):
    @pl.when(pl.program_id(1) == 0)
    def _():
        carry_scr[...] = jnp.zeros(carry_scr.shape, F32)

    h = _rms_norm(x_ref[0], g_ref[...]).astype(BF16)

    rid = lax.broadcasted_iota(jnp.int32, (V7X_MXU_DIM, V7X_MXU_DIM), 0) // HEAD_DIM
    cid = lax.broadcasted_iota(jnp.int32, (V7X_MXU_DIM, V7X_MXU_DIM), 1) // HEAD_DIM
    head_mean = jnp.where(rid == cid, 1.0 / HEAD_DIM, 0.0).astype(BF16)

    def head_norm(t, gain):
        sq = t * t
        hi = sq.astype(BF16)
        lo = (sq - hi.astype(F32)).astype(BF16)
        parts = []
        for j in range(d // V7X_MXU_DIM):
            cols = slice(j * V7X_MXU_DIM, (j + 1) * V7X_MXU_DIM)
            parts.append(_dot(hi[:, cols], head_mean) + _dot(lo[:, cols], head_mean))
        ms = jnp.concatenate(parts, axis=1)
        return t * lax.rsqrt(ms + EPS) * gain

    q = head_norm(_dot(h, wqkv_ref[:, 0:d]), qg_ref[...])
    qt_ref[0] = (q * (HEAD_DIM ** -0.5 * LOG2_E)).T.astype(BF16)
    k = head_norm(_dot(h, wqkv_ref[:, d:2 * d]), kg_ref[...])
    k_ref[0] = k.astype(BF16)
    vt_ref[0] = _dot(h, wqkv_ref[:, 2 * d:3 * d]).T.astype(BF16)

    c = _log_sigmoid(_dot(h, wf_ref[...]) + bf_ref[...])
    row = lax.broadcasted_iota(jnp.int32, c.shape, 0)
    step = 1
    while step < ts:
        c = c + jnp.where(row >= step, pltpu.roll(c, step, 0), 0.0)
        step *= 2
    c = c + carry_scr[...]
    carry_scr[...] = c[ts - 1:, :]

    c2 = c * LOG2_E
    hi = c2.astype(BF16)
    mid = (c2 - hi.astype(F32)).astype(BF16)
    lo = (c2 - hi.astype(F32) - mid.astype(F32)).astype(BF16)
    pieces = jnp.concatenate([hi, mid, lo], axis=1)
    src = lax.broadcasted_iota(jnp.int32, (BIAS_PIECES * V7X_LANES, 2 * V7X_LANES), 0)
    dst = lax.broadcasted_iota(jnp.int32, (BIAS_PIECES * V7X_LANES, 2 * V7X_LANES), 1)
    piece, head = src // V7X_LANES, src % V7X_LANES
    q_slot = BIAS_ROWS * head + piece
    k_slot = V7X_LANES + BIAS_ROWS * head + BIAS_PIECES + piece
    place = jnp.where(head < N_HEADS,
                      jnp.where(dst == q_slot, 1.0, jnp.where(dst == k_slot, -1.0, 0.0)),
                      0.0).astype(BF16)
    lane = lax.broadcasted_iota(jnp.int32, (1, 2 * V7X_LANES), 1)
    slot = lane % V7X_LANES
    is_ones = (slot < BIAS_ROWS * N_HEADS) & (
        ((lane < V7X_LANES) & (slot % BIAS_ROWS >= BIAS_PIECES))
        | ((lane >= V7X_LANES) & (slot % BIAS_ROWS < BIAS_PIECES)))
    bias = _dot(pieces, place) + jnp.where(is_ones, 1.0, 0.0)
    qb_ref[0] = bias[:, :V7X_LANES].T.astype(BF16)
    kb_ref[0] = bias[:, V7X_LANES:].astype(BF16)


def _fox_in(x, g, wqkv, wf, bf, qg, kg, *, ts):
    bsz, seq, d = x.shape
    kern = functools.partial(_fox_in_kernel, ts=ts, d=d)
    tok = pl.BlockSpec((1, ts, d), lambda b, s: (b, s, 0))
    tok_t = pl.BlockSpec((1, d, ts), lambda b, s: (b, 0, s))
    return pl.pallas_call(
        kern,
        grid=(bsz, seq // ts),
        in_specs=[tok, _resident(g.shape), _resident(wqkv.shape), _resident(wf.shape),
                  _resident(bf.shape), _resident(qg.shape), _resident(kg.shape)],
        out_specs=[tok_t, tok, tok_t,
                   pl.BlockSpec((1, V7X_LANES, ts), lambda b, s: (b, 0, s)),
                   pl.BlockSpec((1, ts, V7X_LANES), lambda b, s: (b, s, 0))],
        out_shape=[jax.ShapeDtypeStruct((bsz, d, seq), BF16),
                   jax.ShapeDtypeStruct((bsz, seq, d), BF16),
                   jax.ShapeDtypeStruct((bsz, d, seq), BF16),
                   jax.ShapeDtypeStruct((bsz, V7X_LANES, seq), BF16),
                   jax.ShapeDtypeStruct((bsz, seq, V7X_LANES), BF16)],
        scratch_shapes=[pltpu.VMEM((1, V7X_LANES), F32)],
        compiler_params=pltpu.CompilerParams(
            dimension_semantics=("parallel", "arbitrary"),
            vmem_limit_bytes=V7X_VMEM_LIMIT_BYTES),
        name="fox_in",
    )(x, g, wqkv, wf, bf, qg, kg)


def _fox_attn_kernel(qt_ref, k_ref, kb_ref, vt_ref, qb_ref, o_ref, s_scr, *, tq, tk):
    hp = pl.program_id(1)
    qi = pl.program_id(2)
    qt = qt_ref[0]
    qb = qb_ref[0]
    row = lax.broadcasted_iota(jnp.int32, (V7X_LANES, tq), 0)
    zero = jnp.zeros_like(qt)
    rhs = []
    for hh in range(HEADS_PER_STEP):
        first = (hp * HEADS_PER_STEP + hh) * BIAS_ROWS
        q_rows = jnp.where((row >= hh * HEAD_DIM) & (row < (hh + 1) * HEAD_DIM), qt, zero)
        b_rows = jnp.where((row >= first) & (row < first + BIAS_ROWS), qb, zero)
        rhs.append(jnp.concatenate([q_rows, b_rows], axis=0))
    rhs = jnp.concatenate(rhs, axis=1)

    causal = (lax.broadcasted_iota(jnp.int32, (tk, tk), 0)
              <= lax.broadcasted_iota(jnp.int32, (tk, tk), 1))
    ones_rows = jnp.ones((BF16_SUBLANES, tk), BF16)

    def scores(j, rhs_cols):
        keys = pl.ds(pl.multiple_of(j * tk, tk), tk)
        lhs = jnp.concatenate([k_ref[0, keys, :], kb_ref[0, keys, :]], axis=1)
        return _dot(lhs, rhs_cols)

    def col_max(s):
        return [jnp.max(s[:, hh * tq:(hh + 1) * tq], axis=0, keepdims=True)
                for hh in range(HEADS_PER_STEP)]

    def attend(j, s_heads, carry, lo, hi, *, diagonal, s_max=None):
        keys = pl.ds(pl.multiple_of(j * tk, tk), tk)
        out = []
        for hh in range(HEADS_PER_STEP):
            m, acc = carry[2 * hh:2 * hh + 2]
            sh = s_heads[hh]
            if diagonal:
                sh = jnp.where(causal, sh, NEG_INF)
            tile_max = jnp.max(sh, axis=0, keepdims=True) if s_max is None else s_max[hh]
            m_new = jnp.maximum(m[:, lo:hi], tile_max)
            p = jnp.exp2(sh - m_new).astype(BF16)
            vt = jnp.concatenate(
                [vt_ref[0, hh * HEAD_DIM:(hh + 1) * HEAD_DIM, keys], ones_rows], axis=0)
            acc_new = jnp.exp2(m[:, lo:hi] - m_new) * acc[:, lo:hi] + _dot(vt, p)
            if (lo, hi) != (0, tq):
                keep = [(0, lo), None, (hi, tq)]
                m_new = jnp.concatenate([m_new if c is None else m[:, c[0]:c[1]]
                                         for c in keep if c is None or c[0] < c[1]], axis=1)
                acc_new = jnp.concatenate([acc_new if c is None else acc[:, c[0]:c[1]]
                                           for c in keep if c is None or c[0] < c[1]], axis=1)
            out += [m_new, acc_new]
        return tuple(out)

    def heads_of(s_buf, width):
        return [s_buf[:, hh * width:(hh + 1) * width] for hh in range(HEADS_PER_STEP)]

    assert tq == 2 * tk
    n_full = 2 * qi
    carry = (jnp.full((1, tq), NEG_INF, F32),
             jnp.zeros((HEAD_DIM + BF16_SUBLANES, tq), F32)) * HEADS_PER_STEP
    s_first = scores(0, rhs)
    s_scr[0] = s_first

    def tile_pair(jp, state):
        carry, max0 = state[:-HEADS_PER_STEP], state[-HEADS_PER_STEP:]
        j = 2 * jp
        s1 = scores(j + 1, rhs)
        s_scr[1] = s1
        max1 = col_max(s1)
        carry = attend(j, heads_of(s_scr.at[0], tq), carry, 0, tq, diagonal=False, s_max=max0)
        s0 = scores(j + 2, rhs)
        s_scr[0] = s0
        max0 = col_max(s0)
        carry = attend(j + 1, heads_of(s_scr.at[1], tq), carry, 0, tq, diagonal=False, s_max=max1)
        return carry + tuple(max0)

    carry = lax.fori_loop(0, qi, tile_pair, carry + tuple(col_max(s_first)))[:-HEADS_PER_STEP]
    rhs_late = jnp.concatenate([rhs[:, hh * tq + tk:(hh + 1) * tq]
                                for hh in range(HEADS_PER_STEP)], axis=1)
    s_scr[1, :, 0:HEADS_PER_STEP * tk] = scores(n_full + 1, rhs_late)
    s0 = s_scr.at[0]
    carry = attend(n_full, [s0[:, hh * tq:hh * tq + tk] for hh in range(HEADS_PER_STEP)],
                   carry, 0, tk, diagonal=True)
    carry = attend(n_full, [s0[:, hh * tq + tk:(hh + 1) * tq] for hh in range(HEADS_PER_STEP)],
                   carry, tk, tq, diagonal=False)
    carry = attend(n_full + 1, heads_of(s_scr.at[1], tk), carry, tk, tq, diagonal=True)
    o_t = jnp.concatenate([carry[2 * hh + 1][:HEAD_DIM] / carry[2 * hh + 1][HEAD_DIM:HEAD_DIM + 1]
                           for hh in range(HEADS_PER_STEP)], axis=0)
    o_ref[0] = o_t.T.astype(BF16)


def _fox_attn(qt, k, kb, vt, qb, *, tq, tk):
    bsz, seq, d = k.shape
    kern = functools.partial(_fox_attn_kernel, tq=tq, tk=tk)
    return pl.pallas_call(
        kern,
        grid=(bsz, d // V7X_LANES, seq // tq),
        in_specs=[pl.BlockSpec((1, V7X_LANES, tq), lambda b, hp, i: (b, hp, i)),
                  pl.BlockSpec((1, seq, V7X_LANES), lambda b, hp, i: (b, 0, hp)),
                  pl.BlockSpec((1, seq, V7X_LANES), lambda b, hp, i: (b, 0, 0)),
                  pl.BlockSpec((1, V7X_LANES, seq), lambda b, hp, i: (b, hp, 0)),
                  pl.BlockSpec((1, V7X_LANES, tq), lambda b, hp, i: (b, 0, i))],
        out_specs=pl.BlockSpec((1, tq, V7X_LANES), lambda b, hp, i: (b, i, hp)),
        out_shape=jax.ShapeDtypeStruct((bsz, seq, d), BF16),
        scratch_shapes=[pltpu.VMEM((2, tk, HEADS_PER_STEP * tq), F32)],
        compiler_params=pltpu.CompilerParams(
            dimension_semantics=("parallel", "parallel", "arbitrary"),
            vmem_limit_bytes=V7X_VMEM_LIMIT_BYTES),
        name="fox_attn",
    )(qt, k, kb, vt, qb)


def _block_diag_tiles(w):
    n_tiles = LRU_BLOCKS // HEADS_PER_MXU_TILE
    w4 = w.reshape(n_tiles, HEADS_PER_MXU_TILE, HEAD_DIM, HEAD_DIM)
    eye = jnp.eye(HEADS_PER_MXU_TILE, dtype=w.dtype)
    return jnp.einsum("jade,ab->jadbe", w4, eye).reshape(n_tiles, V7X_MXU_DIM, V7X_MXU_DIM)


def kernel(x, mix_norm, mlp_norm, mlp_w1, mlp_w2, lru_w_in, lru_conv_w, lru_conv_b, lru_w_r, lru_b_r, lru_w_i, lru_b_i, lru_lambda, lru_w_out, fox_w_in, fox_b_f, fox_q_gain, fox_k_gain, fox_w_out):
    bsz, seq, d = x.shape
    t = bsz * seq
    assert d == N_HEADS * HEAD_DIM and lru_w_r.shape[1:] == (LRU_BLOCKS, HEAD_DIM, HEAD_DIM)
    ts = min(256, seq)
    tt = min(32, seq)
    tq = min(1024, seq)
    tk = tq // 2
    tm = min(512, t)
    tf = 1024
    assert bsz % V7X_SUBLANES == 0 and seq % tt == 0 and tt % V7X_SUBLANES == 0
    assert seq % ts == 0 and seq % tq == 0 and tq % tk == 0
    assert t % tm == 0 and mlp_w1.shape[2] % tf == 0

    row = lambda v: v.reshape(1, -1).astype(F32)

    w_gates = jnp.concatenate(
        [_block_diag_tiles(lru_w_r[0]), _block_diag_tiles(lru_w_i[0])], axis=2).astype(BF16)
    b_gates = jnp.stack([lru_b_r[0].reshape(-1), lru_b_i[0].reshape(-1)]).astype(F32)
    y = _lru_mixer(x, row(mix_norm[0]), lru_w_in[0].astype(BF16), lru_conv_w[0].astype(F32),
                   row(lru_conv_b[0]), w_gates, b_gates, row(lru_lambda[0]), tt=tt)
    x = _mlp_block(x.reshape(t, d), y.reshape(t, d), lru_w_out[0].astype(BF16),
                   row(mlp_norm[0]), mlp_w1[0].astype(BF16), mlp_w2[0].astype(BF16),
                   tm=tm, tf=tf).reshape(bsz, seq, d)

    w_in = fox_w_in[0]
    wf = jnp.pad(w_in[:, 3 * d:], ((0, 0), (0, V7X_LANES - N_HEADS))).astype(BF16)
    bf = jnp.pad(fox_b_f[0], (0, V7X_LANES - N_HEADS)).reshape(1, -1).astype(F32)
    qt, k, vt, qb, kb = _fox_in(x, row(mix_norm[1]), w_in[:, :3 * d].astype(BF16), wf, bf,
                                row(jnp.tile(fox_q_gain[0], N_HEADS)),
                                row(jnp.tile(fox_k_gain[0], N_HEADS)), ts=ts)
    o = _fox_attn(qt, k, kb, vt, qb, tq=tq, tk=tk)
    x = _mlp_block(x.reshape(t, d), o.reshape(t, d), fox_w_out[0].astype(BF16),
                   row(mlp_norm[1]), mlp_w1[1].astype(BF16), mlp_w2[1].astype(BF16),
                   tm=tm, tf=tf).reshape(bsz, seq, d)
    return x
```

```python
import functools

import jax
import jax.numpy as jnp
from jax import lax
from jax.experimental import pallas as pl
from jax.experimental.pallas import tpu as pltpu

F32 = jnp.float32
BF16 = jnp.bfloat16

EPS = 1e-6
NEG_INF = -1e30
N_HEADS = 16
HEAD_DIM = 64
LRU_BLOCKS = 16
CONV_WIDTH = 4
LRU_C = 8.0
LOG2_E = 1.4426950408889634
BIAS_PIECES = 3
BIAS_ROWS = 2 * BIAS_PIECES

V7X_SUBLANES = 8
BF16_SUBLANES = 2 * V7X_SUBLANES
V7X_LANES = 128
V7X_MXU_DIM = 256
V7X_VMEM_LIMIT_BYTES = 56 * 1024 * 1024

HEADS_PER_STEP = V7X_LANES // HEAD_DIM
HEADS_PER_MXU_TILE = V7X_MXU_DIM // HEAD_DIM


def _rms_norm(x, g):
    return x * lax.rsqrt(jnp.mean(x * x, axis=-1, keepdims=True) + EPS) * g


def _gelu_tanh(x):
    return 0.5 * x * (1.0 + jnp.tanh(0.7978845608028654 * (x + 0.044715 * (x * x * x))))


def _log_sigmoid(x):
    return jnp.minimum(x, 0.0) - jnp.log1p(jnp.exp(-jnp.abs(x)))


def _dot(a, b):
    return jnp.dot(a, b, preferred_element_type=F32)


def _resident(shape):
    zeros = (0,) * len(shape)
    return pl.BlockSpec(shape, lambda *_: zeros, pipeline_mode=pl.Buffered(1))


def _lru_kernel(x_ref, g_ref, win_ref, cw_ref, cb_ref, wg_ref, bg_ref, lam_ref, y_ref,
                xb_scr, h_scr, *, tt, d):
    nb = V7X_SUBLANES
    rows = nb * tt
    tail = (CONV_WIDTH - 1) * nb

    @pl.when(pl.program_id(1) == 0)
    def _():
        xb_scr[0:tail, :] = jnp.zeros((tail, d), F32)
        h_scr[...] = jnp.zeros((nb, d), F32)

    r_id = lax.broadcasted_iota(jnp.int32, (rows, rows), 0)
    c_id = lax.broadcasted_iota(jnp.int32, (rows, rows), 1)
    to_time_major = jnp.where(c_id == (r_id % nb) * tt + r_id // nb, 1.0, 0.0).astype(BF16)
    to_seq_major = jnp.where(r_id == (c_id % nb) * tt + c_id // nb, 1.0, 0.0).astype(BF16)

    h = _rms_norm(x_ref[...].reshape(rows, d), g_ref[...]).astype(BF16)
    h = _dot(to_time_major, h).astype(BF16)
    u = _dot(h, win_ref[...])
    gate = _gelu_tanh(u[:, :d])
    xb = u[:, d:]

    xb_scr[tail:tail + rows, :] = xb
    xc = cb_ref[...] + cw_ref[CONV_WIDTH - 1:CONV_WIDTH, :] * xb
    for back in range(1, CONV_WIDTH):
        k = CONV_WIDTH - 1 - back
        xc = xc + cw_ref[k:k + 1, :] * xb_scr[tail - back * nb:tail - back * nb + rows, :]
    xb_scr[0:tail, :] = xb[rows - tail:, :]

    xcb = xc.astype(BF16)
    r_parts, i_parts = [], []
    for j in range(d // V7X_MXU_DIM):
        lo = j * V7X_MXU_DIM
        g = _dot(xcb[:, lo:lo + V7X_MXU_DIM], wg_ref[j])
        r_parts.append(g[:, :V7X_MXU_DIM])
        i_parts.append(g[:, V7X_MXU_DIM:])
    r = jax.nn.sigmoid(jnp.concatenate(r_parts, axis=1) + bg_ref[0:1, :])
    i = jax.nn.sigmoid(jnp.concatenate(i_parts, axis=1) + bg_ref[1:2, :])

    neg_rate = -LRU_C * _log_sigmoid(lam_ref[...])
    neg_log_a = r * neg_rate
    a = jnp.exp2(r * (neg_rate * -LOG2_E))
    gain_sq = jnp.tanh(neg_log_a) * (1.0 + a * a)
    gain = jnp.where(gain_sq > 0.0, gain_sq * lax.rsqrt(gain_sq), 0.0)
    b = gain * (i * xc)

    state = h_scr[...]
    hs = []
    for t in range(tt):
        state = a[t * nb:(t + 1) * nb, :] * state + b[t * nb:(t + 1) * nb, :]
        hs.append(state)
    h_scr[...] = state
    y = (gate * jnp.concatenate(hs, axis=0)).astype(BF16)
    y_ref[...] = _dot(to_seq_major, y).astype(BF16).reshape(nb, tt, d)


def _lru_mixer(x, g, w_in, conv_w, conv_b, w_gates, b_gates, lam, *, tt):
    bsz, seq, d = x.shape
    nb = V7X_SUBLANES
    kern = functools.partial(_lru_kernel, tt=tt, d=d)
    return pl.pallas_call(
        kern,
        grid=(bsz // nb, seq // tt),
        in_specs=[
            pl.BlockSpec((nb, tt, d), lambda b, s: (b, s, 0)),
            _resident(g.shape), _resident(w_in.shape), _resident(conv_w.shape),
            _resident(conv_b.shape), _resident(w_gates.shape), _resident(b_gates.shape),
            _resident(lam.shape),
        ],
        out_specs=pl.BlockSpec((nb, tt, d), lambda b, s: (b, s, 0)),
        out_shape=jax.ShapeDtypeStruct((bsz, seq, d), BF16),
        scratch_shapes=[
            pltpu.VMEM(((tt + CONV_WIDTH - 1) * nb, d), F32),
            pltpu.VMEM((nb, d), F32),
        ],
        compiler_params=pltpu.CompilerParams(
            dimension_semantics=("parallel", "arbitrary"),
            vmem_limit_bytes=V7X_VMEM_LIMIT_BYTES),
        name="lru_mixer",
    )(x, g, w_in, conv_w, conv_b, w_gates, b_gates, lam)


def _mlp_kernel(x_ref, y_ref, wout_ref, g_ref, w1_ref, w2_ref, o_ref, mid_scr, *, tf):
    x = x_ref[...] + _dot(y_ref[...], wout_ref[...])
    h = _rms_norm(x, g_ref[...]).astype(BF16)
    d_ff = w1_ref.shape[1]
    for c in range(d_ff // tf):
        mid = _dot(h, w1_ref[:, c * tf:(c + 1) * tf])
        mid_scr[:, c * tf:(c + 1) * tf] = jnp.square(jnp.maximum(mid, 0.0)).astype(BF16)
    o_ref[...] = x + _dot(mid_scr[...], w2_ref[...])


def _mlp_block(x, y, w_out, g, w1, w2, *, tm, tf):
    t, d = x.shape
    d_ff = w1.shape[1]
    kern = functools.partial(_mlp_kernel, tf=tf)
    return pl.pallas_call(
        kern,
        grid=(t // tm,),
        in_specs=[
            pl.BlockSpec((tm, d), lambda i: (i, 0)),
            pl.BlockSpec((tm, d), lambda i: (i, 0)),
            _resident(w_out.shape), _resident(g.shape), _resident(w1.shape), _resident(w2.shape),
        ],
        out_specs=pl.BlockSpec((tm, d), lambda i: (i, 0)),
        out_shape=jax.ShapeDtypeStruct((t, d), F32),
        scratch_shapes=[pltpu.VMEM((tm, d_ff), BF16)],
        compiler_params=pltpu.CompilerParams(
            dimension_semantics=("parallel",),
            vmem_limit_bytes=V7X_VMEM_LIMIT_BYTES),
        name="mlp_block",
    )(x, y, w_out, g, w1, w2)


def _fox_in_kernel(x_ref, g_ref, wqkv_ref, wf_ref, bf_ref, qg_ref, kg_ref,
                   qt_ref, k_ref, vt_ref, qb_ref, kb_ref, carry_scr, *, ts, d):
    @pl.when(pl.program_id(1) == 0)
    def _():
        carry_scr[...] = jnp.zeros(carry_scr.shape, F32)

    h = _rms_norm(x_ref[0], g_ref[...]).astype(BF16)

    def head_norm_t(t, gain_ref):
        t3 = t.T.reshape(N_HEADS, HEAD_DIM, ts)
        ms = jnp.mean(t3 * t3, axis=1, keepdims=True)
        gain = jnp.concatenate([gain_ref[...]] * (ts // V7X_LANES), axis=1)
        return (t3 * lax.rsqrt(ms + EPS)).reshape(d, ts) * gain

    qt = head_norm_t(_dot(h, wqkv_ref[:, 0:d]), qg_ref)
    qt_ref[0] = (qt * (HEAD_DIM ** -0.5 * LOG2_E)).astype(BF16)
    k_ref[0] = head_norm_t(_dot(h, wqkv_ref[:, d:2 * d]), kg_ref).T.astype(BF16)
    vt_ref[0] = _dot(h, wqkv_ref[:, 2 * d:3 * d]).T.astype(BF16)

    c = _log_sigmoid(_dot(h, wf_ref[...]) + bf_ref[...])
    row = lax.broadcasted_iota(jnp.int32, c.shape, 0)
    step = 1
    while step < ts:
        c = c + jnp.where(row >= step, pltpu.roll(c, step, 0), 0.0)
        step *= 2
    c = c + carry_scr[...]
    carry_scr[...] = c[ts - 1:, :]

    c2 = c * LOG2_E
    hi = c2.astype(BF16)
    mid = (c2 - hi.astype(F32)).astype(BF16)
    lo = (c2 - hi.astype(F32) - mid.astype(F32)).astype(BF16)
    pieces = jnp.concatenate([hi, mid, lo], axis=1)
    src = lax.broadcasted_iota(jnp.int32, (BIAS_PIECES * V7X_LANES, 2 * V7X_LANES), 0)
    dst = lax.broadcasted_iota(jnp.int32, (BIAS_PIECES * V7X_LANES, 2 * V7X_LANES), 1)
    piece, head = src // V7X_LANES, src % V7X_LANES
    q_slot = BIAS_ROWS * head + piece
    k_slot = V7X_LANES + BIAS_ROWS * head + BIAS_PIECES + piece
    place = jnp.where(head < N_HEADS,
                      jnp.where(dst == q_slot, 1.0, jnp.where(dst == k_slot, -1.0, 0.0)),
                      0.0).astype(BF16)
    lane = lax.broadcasted_iota(jnp.int32, (1, 2 * V7X_LANES), 1)
    slot = lane % V7X_LANES
    is_ones = (slot < BIAS_ROWS * N_HEADS) & (
        ((lane < V7X_LANES) & (slot % BIAS_ROWS >= BIAS_PIECES))
        | ((lane >= V7X_LANES) & (slot % BIAS_ROWS < BIAS_PIECES)))
    bias = _dot(pieces, place) + jnp.where(is_ones, 1.0, 0.0)
    qb_ref[0] = bias[:, :V7X_LANES].T.astype(BF16)
    kb_ref[0] = bias[:, V7X_LANES:].astype(BF16)


def _fox_in(x, g, wqkv, wf, bf, qg, kg, *, ts):
    bsz, seq, d = x.shape
    kern = functools.partial(_fox_in_kernel, ts=ts, d=d)
    tok = pl.BlockSpec((1, ts, d), lambda b, s: (b, s, 0))
    tok_t = pl.BlockSpec((1, d, ts), lambda b, s: (b, 0, s))
    return pl.pallas_call(
        kern,
        grid=(bsz, seq // ts),
        in_specs=[tok, _resident(g.shape), _resident(wqkv.shape), _resident(wf.shape),
                  _resident(bf.shape), _resident(qg.shape), _resident(kg.shape)],
        out_specs=[tok_t, tok, tok_t,
                   pl.BlockSpec((1, V7X_LANES, ts), lambda b, s: (b, 0, s)),
                   pl.BlockSpec((1, ts, V7X_LANES), lambda b, s: (b, s, 0))],
        out_shape=[jax.ShapeDtypeStruct((bsz, d, seq), BF16),
                   jax.ShapeDtypeStruct((bsz, seq, d), BF16),
                   jax.ShapeDtypeStruct((bsz, d, seq), BF16),
                   jax.ShapeDtypeStruct((bsz, V7X_LANES, seq), BF16),
                   jax.ShapeDtypeStruct((bsz, seq, V7X_LANES), BF16)],
        scratch_shapes=[pltpu.VMEM((1, V7X_LANES), F32)],
        compiler_params=pltpu.CompilerParams(
            dimension_semantics=("parallel", "arbitrary"),
            vmem_limit_bytes=V7X_VMEM_LIMIT_BYTES),
        name="fox_in",
    )(x, g, wqkv, wf, bf, qg, kg)


def _fox_attn_kernel(qt_ref, k_ref, kb_ref, vt_ref, qb_ref, o_ref, s_scr, *, tq, tk):
    hp = pl.program_id(1)
    qi = pl.program_id(2)
    qt = qt_ref[0]
    qb = qb_ref[0]
    row = lax.broadcasted_iota(jnp.int32, (V7X_LANES, tq), 0)
    zero = jnp.zeros_like(qt)
    rhs = []
    for hh in range(HEADS_PER_STEP):
        first = (hp * HEADS_PER_STEP + hh) * BIAS_ROWS
        q_rows = jnp.where((row >= hh * HEAD_DIM) & (row < (hh + 1) * HEAD_DIM), qt, zero)
        b_rows = jnp.where((row >= first) & (row < first + BIAS_ROWS), qb, zero)
        rhs.append(jnp.concatenate([q_rows, b_rows], axis=0))
    rhs = jnp.concatenate(rhs, axis=1)

    causal = (lax.broadcasted_iota(jnp.int32, (tk, tk), 0)
              <= lax.broadcasted_iota(jnp.int32, (tk, tk), 1))
    ones_rows = jnp.ones((BF16_SUBLANES, tk), BF16)

    def scores(j, rhs_cols):
        keys = pl.ds(pl.multiple_of(j * tk, tk), tk)
        lhs = jnp.concatenate([k_ref[0, keys, :], kb_ref[0, keys, :]], axis=1)
        return _dot(lhs, rhs_cols)

    def col_max(s):
        return [jnp.max(s[:, hh * tq:(hh + 1) * tq], axis=0, keepdims=True)
                for hh in range(HEADS_PER_STEP)]

    def attend(j, s_heads, carry, lo, hi, *, diagonal, s_max=None):
        keys = pl.ds(pl.multiple_of(j * tk, tk), tk)
        out = []
        for hh in range(HEADS_PER_STEP):
            m, acc = carry[2 * hh:2 * hh + 2]
            sh = s_heads[hh]
            if diagonal:
                sh = jnp.where(causal, sh, NEG_INF)
            tile_max = jnp.max(sh, axis=0, keepdims=True) if s_max is None else s_max[hh]
            m_new = jnp.maximum(m[:, lo:hi], tile_max)
            p = jnp.exp2(sh - m_new).astype(BF16)
            vt = jnp.concatenate(
                [vt_ref[0, hh * HEAD_DIM:(hh + 1) * HEAD_DIM, keys], ones_rows], axis=0)
            acc_new = jnp.exp2(m[:, lo:hi] - m_new) * acc[:, lo:hi] + _dot(vt, p)
            if (lo, hi) != (0, tq):
                keep = [(0, lo), None, (hi, tq)]
                m_new = jnp.concatenate([m_new if c is None else m[:, c[0]:c[1]]
                                         for c in keep if c is None or c[0] < c[1]], axis=1)
                acc_new = jnp.concatenate([acc_new if c is None else acc[:, c[0]:c[1]]
                                           for c in keep if c is None or c[0] < c[1]], axis=1)
            out += [m_new, acc_new]
        return tuple(out)

    def heads_of(s_buf, width):
        return [s_buf[:, hh * width:(hh + 1) * width] for hh in range(HEADS_PER_STEP)]

    assert tq == 2 * tk
    n_full = 2 * qi
    carry = (jnp.full((1, tq), NEG_INF, F32),
             jnp.zeros((HEAD_DIM + BF16_SUBLANES, tq), F32)) * HEADS_PER_STEP
    s_first = scores(0, rhs)
    s_scr[0] = s_first

    def tile_pair(jp, state):
        carry, max0 = state[:-HEADS_PER_STEP], state[-HEADS_PER_STEP:]
        j = 2 * jp
        s1 = scores(j + 1, rhs)
        s_scr[1] = s1
        max1 = col_max(s1)
        carry = attend(j, heads_of(s_scr.at[0], tq), carry, 0, tq, diagonal=False, s_max=max0)
        s0 = scores(j + 2, rhs)
        s_scr[0] = s0
        max0 = col_max(s0)
        carry = attend(j + 1, heads_of(s_scr.at[1], tq), carry, 0, tq, diagonal=False, s_max=max1)
        return carry + tuple(max0)

    carry = lax.fori_loop(0, qi, tile_pair, carry + tuple(col_max(s_first)))[:-HEADS_PER_STEP]
    rhs_late = jnp.concatenate([rhs[:, hh * tq + tk:(hh + 1) * tq]
                                for hh in range(HEADS_PER_STEP)], axis=1)
    s_scr[1, :, 0:HEADS_PER_STEP * tk] = scores(n_full + 1, rhs_late)
    s0 = s_scr.at[0]
    carry = attend(n_full, [s0[:, hh * tq:hh * tq + tk] for hh in range(HEADS_PER_STEP)],
                   carry, 0, tk, diagonal=True)
    carry = attend(n_full, [s0[:, hh * tq + tk:(hh + 1) * tq] for hh in range(HEADS_PER_STEP)],
                   carry, tk, tq, diagonal=False)
    carry = attend(n_full + 1, heads_of(s_scr.at[1], tk), carry, tk, tq, diagonal=True)
    o_t = jnp.concatenate([carry[2 * hh + 1][:HEAD_DIM] / carry[2 * hh + 1][HEAD_DIM:HEAD_DIM + 1]
                           for hh in range(HEADS_PER_STEP)], axis=0)
    o_ref[0] = o_t.T.astype(BF16)


def _fox_attn(qt, k, kb, vt, qb, *, tq, tk):
    bsz, seq, d = k.shape
    kern = functools.partial(_fox_attn_kernel, tq=tq, tk=tk)
    return pl.pallas_call(
        kern,
        grid=(bsz, d // V7X_LANES, seq // tq),
        in_specs=[pl.BlockSpec((1, V7X_LANES, tq), lambda b, hp, i: (b, hp, i)),
                  pl.BlockSpec((1, seq, V7X_LANES), lambda b, hp, i: (b, 0, hp)),
                  pl.BlockSpec((1, seq, V7X_LANES), lambda b, hp, i: (b, 0, 0)),
                  pl.BlockSpec((1, V7X_LANES, seq), lambda b, hp, i: (b, hp, 0)),
                  pl.BlockSpec((1, V7X_LANES, tq), lambda b, hp, i: (b, 0, i))],
        out_specs=pl.BlockSpec((1, tq, V7X_LANES), lambda b, hp, i: (b, i, hp)),
        out_shape=jax.ShapeDtypeStruct((bsz, seq, d), BF16),
        scratch_shapes=[pltpu.VMEM((2, tk, HEADS_PER_STEP * tq), F32)],
        compiler_params=pltpu.CompilerParams(
            dimension_semantics=("parallel", "parallel", "arbitrary"),
            vmem_limit_bytes=V7X_VMEM_LIMIT_BYTES),
        name="fox_attn",
    )(qt, k, kb, vt, qb)


def _block_diag_tiles(w):
    n_tiles = LRU_BLOCKS // HEADS_PER_MXU_TILE
    w4 = w.reshape(n_tiles, HEADS_PER_MXU_TILE, HEAD_DIM, HEAD_DIM)
    eye = jnp.eye(HEADS_PER_MXU_TILE, dtype=w.dtype)
    return jnp.einsum("jade,ab->jadbe", w4, eye).reshape(n_tiles, V7X_MXU_DIM, V7X_MXU_DIM)


def kernel(x, mix_norm, mlp_norm, mlp_w1, mlp_w2, lru_w_in, lru_conv_w, lru_conv_b, lru_w_r, lru_b_r, lru_w_i, lru_b_i, lru_lambda, lru_w_out, fox_w_in, fox_b_f, fox_q_gain, fox_k_gain, fox_w_out):
    bsz, seq, d = x.shape
    t = bsz * seq
    assert d == N_HEADS * HEAD_DIM and lru_w_r.shape[1:] == (LRU_BLOCKS, HEAD_DIM, HEAD_DIM)
    ts = min(512, seq)
    tt = min(32, seq)
    tq = min(1024, seq)
    tk = tq // 2
    tm = min(512, t)
    tf = 1024
    assert bsz % V7X_SUBLANES == 0 and seq % tt == 0 and tt % V7X_SUBLANES == 0
    assert seq % ts == 0 and seq % tq == 0 and tq % tk == 0
    assert t % tm == 0 and mlp_w1.shape[2] % tf == 0

    row = lambda v: v.reshape(1, -1).astype(F32)
    gain_cols = lambda v: jnp.broadcast_to(
        jnp.tile(v, N_HEADS).astype(F32)[:, None], (d, V7X_LANES))

    w_gates = jnp.concatenate(
        [_block_diag_tiles(lru_w_r[0]), _block_diag_tiles(lru_w_i[0])], axis=2).astype(BF16)
    b_gates = jnp.stack([lru_b_r[0].reshape(-1), lru_b_i[0].reshape(-1)]).astype(F32)
    y = _lru_mixer(x, row(mix_norm[0]), lru_w_in[0].astype(BF16), lru_conv_w[0].astype(F32),
                   row(lru_conv_b[0]), w_gates, b_gates, row(lru_lambda[0]), tt=tt)
    x = _mlp_block(x.reshape(t, d), y.reshape(t, d), lru_w_out[0].astype(BF16),
                   row(mlp_norm[0]), mlp_w1[0].astype(BF16), mlp_w2[0].astype(BF16),
                   tm=tm, tf=tf).reshape(bsz, seq, d)

    w_in = fox_w_in[0]
    wf = jnp.pad(w_in[:, 3 * d:], ((0, 0), (0, V7X_LANES - N_HEADS))).astype(BF16)
    bf = jnp.pad(fox_b_f[0], (0, V7X_LANES - N_HEADS)).reshape(1, -1).astype(F32)
    qt, k, vt, qb, kb = _fox_in(x, row(mix_norm[1]), w_in[:, :3 * d].astype(BF16), wf, bf,
                                gain_cols(fox_q_gain[0]), gain_cols(fox_k_gain[0]), ts=ts)
    o = _fox_attn(qt, k, kb, vt, qb, tq=tq, tk=tk)
    x = _mlp_block(x.reshape(t, d), o.reshape(t, d), fox_w_out[0].astype(BF16),
                   row(mlp_norm[1]), mlp_w1[1].astype(BF16), mlp_w2[1].astype(BF16),
                   tm=tm, tf=tf).reshape(bsz, seq, d)
    return x
```
